```python
import math
import jax, jax.numpy as jnp
from jax import lax
import numpy as np

D_MODEL = 1024
BATCH = 8
SEQ = 2048
DEPTH = 2
DEC_BATCH = 8
DEC_SEQ = 8192
PAST_LEN = 128

N_EVEN = (DEPTH + 1) // 2
N_ODD = DEPTH // 2
D_FF = 4 * D_MODEL
EPS = 1e-6

POOL_WINDOWS = (2, 4, 8, 16)
POOL_GROUP = D_MODEL // 8
D_POOL = POOL_GROUP * len(POOL_WINDOWS)
MLA_HEADS = 8
QK_NOPE = 64
QK_ROPE = 32
V_HEAD = 64
Q_LORA = D_MODEL // 4
KV_LORA = D_MODEL // 8
ROPE_THETA = 10000.0
Q_BLOCK = 128
D_MLA_OUT = MLA_HEADS * V_HEAD
D_IN_EVEN = D_POOL + Q_LORA + KV_LORA + QK_ROPE
D_HYENA = 3 * D_MODEL // 4
HYENA_ORDER = 2
SHORT_WIDTH = 3
POS_BANDS = 16
POS_EMB = 1 + 2 * POS_BANDS
FILTER_HIDDEN = 64
DECAY_TARGET = 1e-2
FAST_DECAY_PCT = 0.3
SLOW_DECAY_PCT = 1.5
FNET_GROUPS = 4
FNET_GROUP = D_MODEL // 16
D_FNET = FNET_GROUPS * FNET_GROUP
D_IN_ODD = (HYENA_ORDER + 1) * D_HYENA + D_FNET
D_MIX = D_MODEL

kernel_name = "hybrid_pool_mla_hyena_fnet_encoder"


def rms_norm(x, g):
    xf = x.astype(jnp.float32)
    y = xf * lax.rsqrt(jnp.mean(xf * xf, axis=-1, keepdims=True) + EPS)
    return (y * g.astype(jnp.float32)).astype(x.dtype)


def centred_pool_residual(u, window):
    L = u.shape[1]
    before = window // 2
    after = window - 1 - before
    uf = u.astype(jnp.float32)
    up = jnp.pad(uf, ((0, 0), (before + 1, after), (0, 0)))
    c = jnp.cumsum(up, axis=1)
    s = c[:, window:window + L] - c[:, :L]
    pos = jnp.arange(L)
    cnt = (jnp.minimum(pos + after, L - 1) - jnp.maximum(pos - before, 0) + 1).astype(jnp.float32)
    return s / cnt[None, :, None] - uf


def pool_mixer(a, pool_w, pool_scale):
    outs = []
    for gi, w in enumerate(POOL_WINDOWS):
        ag = a[..., gi * POOL_GROUP:(gi + 1) * POOL_GROUP]
        r = centred_pool_residual(ag, w).astype(a.dtype)
        outs.append(jnp.einsum('blc,cd->bld', r, pool_w[gi]))
    return jnp.concatenate(outs, axis=-1) * pool_scale


def rope_tables(L):
    inv = ROPE_THETA ** (-jnp.arange(0, QK_ROPE, 2, dtype=jnp.float32) / QK_ROPE)
    ang = jnp.arange(L, dtype=jnp.float32)[:, None] * inv[None, :]
    return jnp.cos(ang), jnp.sin(ang)


def apply_rope(x, cos, sin):
    xf = x.astype(jnp.float32)
    x1, x2 = jnp.split(xf, 2, axis=-1)
    return jnp.concatenate([x1 * cos - x2 * sin, x2 * cos + x1 * sin], axis=-1).astype(x.dtype)


def mla_mixer(cq, ckv, kr, q_norm, w_uq, kv_norm, w_ukv):
    B, L, _ = cq.shape
    q = jnp.einsum('blr,rf->blf', rms_norm(cq, q_norm), w_uq).reshape(B, L, MLA_HEADS, QK_NOPE + QK_ROPE)
    kv = jnp.einsum('blr,rf->blf', rms_norm(ckv, kv_norm), w_ukv).reshape(B, L, MLA_HEADS, QK_NOPE + V_HEAD)
    q_nope, q_rope = q[..., :QK_NOPE], q[..., QK_NOPE:]
    k_nope, v = kv[..., :QK_NOPE], kv[..., QK_NOPE:]
    cos, sin = rope_tables(L)
    q_rope = apply_rope(q_rope, cos[None, :, None], sin[None, :, None])
    k_rope = apply_rope(kr, cos[None], sin[None])
    scale = (QK_NOPE + QK_ROPE) ** -0.5
    nb = L // Q_BLOCK
    qn_b = q_nope.reshape(B, nb, Q_BLOCK, MLA_HEADS, QK_NOPE).transpose(1, 0, 2, 3, 4)
    qr_b = q_rope.reshape(B, nb, Q_BLOCK, MLA_HEADS, QK_ROPE).transpose(1, 0, 2, 3, 4)

    def block(args):
        qn, qr = args
        s = jnp.einsum('bqhd,bkhd->bhqk', qn, k_nope) + jnp.einsum('bqhr,bkr->bhqk', qr, k_rope)
        p = jax.nn.softmax(s.astype(jnp.float32) * scale, axis=-1).astype(v.dtype)
        return jnp.einsum('bhqk,bkhd->bqhd', p, v)

    o = lax.map(block, (qn_b, qr_b))
    return o.transpose(1, 0, 2, 3, 4).reshape(B, L, D_MLA_OUT)


def even_mixer(h, w_in, pool_w, pool_scale, q_norm, w_uq, kv_norm, w_ukv, w_out):
    z = jnp.einsum('bld,df->blf', h, w_in)
    a = z[..., :D_POOL]
    cq = z[..., D_POOL:D_POOL + Q_LORA]
    ckv = z[..., D_POOL + Q_LORA:D_POOL + Q_LORA + KV_LORA]
    kr = z[..., D_POOL + Q_LORA + KV_LORA:]
    mixed = jnp.concatenate([pool_mixer(a, pool_w, pool_scale),
                             mla_mixer(cq, ckv, kr, q_norm, w_uq, kv_norm, w_ukv)], axis=-1)
    return jnp.einsum('blf,fd->bld', mixed, w_out)


def hyena_filter_spectra(L, w1, b1, w2, b2, w3, freq):
    f32 = lambda a: a.astype(jnp.float32)
    t = jnp.linspace(0.0, 1.0, L, dtype=jnp.float32)
    bands = jnp.linspace(1e-4, POS_BANDS - 1, POS_BANDS, dtype=jnp.float32)
    ang = (2.0 * math.pi / L) * jnp.arange(L, dtype=jnp.float32)[:, None] * bands[None, :]
    feat = jnp.concatenate([t[:, None], jnp.cos(ang), -jnp.sin(ang)], axis=-1)
    fr = f32(freq)
    hid = jnp.sin(fr * (feat @ f32(w1) + f32(b1)))
    hid = jnp.sin(fr * (hid @ f32(w2) + f32(b2)))
    hf = (hid @ f32(w3)).reshape(L, HYENA_ORDER, 2, D_HYENA)
    deltas = jnp.abs(jnp.linspace(math.log(DECAY_TARGET) / SLOW_DECAY_PCT,
                                  math.log(DECAY_TARGET) / FAST_DECAY_PCT, D_HYENA, dtype=jnp.float32))
    hf = hf * jnp.exp(-t[:, None] * deltas[None, :])[:, None, None, :]
    hf = hf * lax.rsqrt(jnp.sum(hf * hf, axis=0, keepdims=True) + EPS)
    h_fwd, h_bwd = hf[:, :, 0], hf[:, :, 1]
    k_full = jnp.concatenate([h_fwd, jnp.zeros((1, HYENA_ORDER, D_HYENA), jnp.float32), h_bwd[:0:-1]], axis=0)
    return jnp.fft.rfft(k_full, axis=0)


def long_conv(u, k_f, bias):
    L = u.shape[1]
    y = jnp.fft.irfft(jnp.fft.rfft(u, n=2 * L, axis=1) * k_f[None], n=2 * L, axis=1)[:, :L]
    return y + u * bias.astype(jnp.float32)


def hyena_mixer(z, short_w, short_b, filt_w1, filt_b1, filt_w2, filt_b2, filt_w3, filt_freq, hyena_bias):
    B, L, _ = z.shape
    pad = SHORT_WIDTH // 2
    zp = jnp.pad(z, ((0, 0), (pad, pad), (0, 0)))
    zc = sum(zp[:, j:j + L] * short_w[j] for j in range(SHORT_WIDTH)) + short_b
    zc = zc.astype(jnp.float32)
    v, x1, x2 = jnp.split(zc, HYENA_ORDER + 1, axis=-1)
    k_f = hyena_filter_spectra(L, filt_w1, filt_b1, filt_w2, filt_b2, filt_w3, filt_freq)
    u = v
    for o, gate in enumerate((x1, x2)):
        u = gate * long_conv(u, k_f[:, o], hyena_bias[o])
    return u.astype(z.dtype)


def fnet_mixer(f, fnet_w):
    B, L, _ = f.shape
    fg = f.astype(jnp.float32).reshape(B, L, FNET_GROUPS, FNET_GROUP)
    m = jnp.fft.fftn(fg, axes=(1, 3), norm='ortho').real.astype(f.dtype)
    return jnp.einsum('blgc,gcd->blgd', m, fnet_w).reshape(B, L, D_FNET)


def odd_mixer(h, w_in, short_w, short_b, filt_w1, filt_b1, filt_w2, filt_b2, filt_w3, filt_freq,
              hyena_bias, fnet_w, w_out):
    z = jnp.einsum('bld,df->blf', h, w_in)
    zh = z[..., :(HYENA_ORDER + 1) * D_HYENA]
    f = z[..., (HYENA_ORDER + 1) * D_HYENA:]
    mixed = jnp.concatenate([
        hyena_mixer(zh, short_w, short_b, filt_w1, filt_b1, filt_w2, filt_b2, filt_w3, filt_freq, hyena_bias),
        fnet_mixer(f, fnet_w)], axis=-1)
    return jnp.einsum('blf,fd->bld', mixed, w_out)


def trunk(x, norm_mix, norm_mlp, w_ff_in, w_ff_out,
          even_w_in, pool_w, pool_scale, mla_q_norm, mla_w_uq, mla_kv_norm, mla_w_ukv, even_w_out,
          odd_w_in, short_w, short_b, filt_w1, filt_b1, filt_w2, filt_b2, filt_w3, filt_freq,
          hyena_bias, fnet_w, odd_w_out):
    for layer in range(DEPTH):
        i = layer // 2
        h = rms_norm(x, norm_mix[layer, 0])
        if layer % 2 == 0:
            m = even_mixer(h, even_w_in[i], pool_w[i], pool_scale[i], mla_q_norm[i], mla_w_uq[i],
                           mla_kv_norm[i], mla_w_ukv[i], even_w_out[i])
        else:
            m = odd_mixer(h, odd_w_in[i], short_w[i], short_b[i], filt_w1[i], filt_b1[i], filt_w2[i],
                          filt_b2[i], filt_w3[i], filt_freq[i], hyena_bias[i], fnet_w[i], odd_w_out[i])
        x = x + rms_norm(m, norm_mix[layer, 1])
        h = rms_norm(x, norm_mlp[layer, 0])
        f = jnp.einsum('bld,df->blf', h, w_ff_in[layer])
        f = jnp.einsum('blf,fd->bld', jnp.square(jax.nn.relu(f)), w_ff_out[layer])
        x = x + rms_norm(f, norm_mlp[layer, 1])
    return x


def setup_inputs(seed: int = 0) -> dict:
    key = jax.random.key(seed)
    ks = iter(jax.random.split(key, 40))
    nrm = lambda shape, scale: scale * jax.random.normal(next(ks), shape, jnp.float32)
    gain = lambda shape: 1.0 + 0.1 * jax.random.normal(next(ks), shape, jnp.float32)
    return {
        "x_prompt": nrm((BATCH, SEQ, D_MODEL), 1.0),
        "x_sample": nrm((DEC_BATCH, DEC_SEQ, D_MODEL), 1.0),
        "norm_mix": gain((DEPTH, 2, D_MODEL)),
        "norm_mlp": gain((DEPTH, 2, D_MODEL)),
        "w_ff_in": nrm((DEPTH, D_MODEL, D_FF), D_MODEL ** -0.5),
        "w_ff_out": nrm((DEPTH, D_FF, D_MODEL), D_FF ** -0.5),
        "even_w_in": nrm((N_EVEN, D_MODEL, D_IN_EVEN), D_MODEL ** -0.5),
        "pool_w": nrm((N_EVEN, len(POOL_WINDOWS), POOL_GROUP, POOL_GROUP), POOL_GROUP ** -0.5),
        "pool_scale": gain((N_EVEN, D_POOL)),
        "mla_q_norm": gain((N_EVEN, Q_LORA)),
        "mla_w_uq": nrm((N_EVEN, Q_LORA, MLA_HEADS * (QK_NOPE + QK_ROPE)), Q_LORA ** -0.5),
        "mla_kv_norm": gain((N_EVEN, KV_LORA)),
        "mla_w_ukv": nrm((N_EVEN, KV_LORA, MLA_HEADS * (QK_NOPE + V_HEAD)), KV_LORA ** -0.5),
        "even_w_out": nrm((N_EVEN, D_MIX, D_MODEL), D_MIX ** -0.5),
        "odd_w_in": nrm((N_ODD, D_MODEL, D_IN_ODD), D_MODEL ** -0.5),
        "short_w": nrm((N_ODD, SHORT_WIDTH, (HYENA_ORDER + 1) * D_HYENA), SHORT_WIDTH ** -0.5),
        "short_b": nrm((N_ODD, (HYENA_ORDER + 1) * D_HYENA), 0.01),
        "filt_w1": nrm((N_ODD, POS_EMB, FILTER_HIDDEN), POS_EMB ** -0.5),
        "filt_b1": nrm((N_ODD, FILTER_HIDDEN), 0.1),
        "filt_w2": nrm((N_ODD, FILTER_HIDDEN, FILTER_HIDDEN), FILTER_HIDDEN ** -0.5),
        "filt_b2": nrm((N_ODD, FILTER_HIDDEN), 0.1),
        "filt_w3": nrm((N_ODD, FILTER_HIDDEN, HYENA_ORDER * 2 * D_HYENA), FILTER_HIDDEN ** -0.5),
        "filt_freq": gain((N_ODD, FILTER_HIDDEN)),
        "hyena_bias": nrm((N_ODD, HYENA_ORDER, D_HYENA), 0.5),
        "fnet_w": nrm((N_ODD, FNET_GROUPS, FNET_GROUP, FNET_GROUP), FNET_GROUP ** -0.5),
        "odd_w_out": nrm((N_ODD, D_MIX, D_MODEL), D_MIX ** -0.5),
    }


def reference(x_prompt, x_sample, norm_mix, norm_mlp, w_ff_in, w_ff_out,
              even_w_in, pool_w, pool_scale, mla_q_norm, mla_w_uq, mla_kv_norm, mla_w_ukv, even_w_out,
              odd_w_in, short_w, short_b, filt_w1, filt_b1, filt_w2, filt_b2, filt_w3, filt_freq,
              hyena_bias, fnet_w, odd_w_out):
    y_prompt = trunk(x_prompt, norm_mix, norm_mlp, w_ff_in, w_ff_out,
                     even_w_in, pool_w, pool_scale, mla_q_norm, mla_w_uq, mla_kv_norm, mla_w_ukv, even_w_out,
                     odd_w_in, short_w, short_b, filt_w1, filt_b1, filt_w2, filt_b2, filt_w3, filt_freq,
                     hyena_bias, fnet_w, odd_w_out)
    y_sample = trunk(x_sample, norm_mix, norm_mlp, w_ff_in, w_ff_out,
                     even_w_in, pool_w, pool_scale, mla_q_norm, mla_w_uq, mla_kv_norm, mla_w_ukv, even_w_out,
                     odd_w_in, short_w, short_b, filt_w1, filt_b1, filt_w2, filt_b2, filt_w3, filt_freq,
                     hyena_bias, fnet_w, odd_w_out)
    return (y_prompt, y_sample)
```

```python
import functools
import math

import numpy as np
import jax
import jax.numpy as jnp
from jax import lax
from jax.experimental import pallas as pl
from jax.experimental.pallas import tpu as pltpu

F32 = jnp.float32
BF16 = jnp.bfloat16
EPS = 1e-6

D_MODEL = 1024
D_FF = 4 * D_MODEL
POOL_WINDOWS = (2, 4, 8, 16)
POOL_GROUP = 128
D_POOL = 512
MLA_HEADS = 8
QK_NOPE = 64
QK_ROPE = 32
V_HEAD = 64
Q_LORA = 256
KV_LORA = 128
ROPE_THETA = 10000.0
HEAD_PAD = 128
D_HYENA = 768
HYENA_ORDER = 2
POS_BANDS = 16
POS_EMB = 1 + 2 * POS_BANDS
FEAT_PAD = 64
FILTER_HIDDEN = 64
DECAY_TARGET = 1e-2
FAST_DECAY_PCT = 0.3
SLOW_DECAY_PCT = 1.5
FNET_GROUPS = 4
FNET_GROUP = 64
D_FNET = 256
D_ZH = 3 * D_HYENA
D_IN_ODD = D_ZH + D_FNET
D_IN_EVEN_PAD = 1024

HALO = 8
FFT_N2 = 128
VMEM_LIMIT = 56 * 1024 * 1024

TM = 512
TQ = 512
TKC = 512
K1C = 4
LANE_CHUNK = 6144


def _cparams(sem):
    return pltpu.CompilerParams(dimension_semantics=sem, vmem_limit_bytes=VMEM_LIMIT)


def _rms(xf, g):
    y = xf * lax.rsqrt(jnp.mean(xf * xf, axis=-1, keepdims=True) + EPS)
    return y * g


def _dot(a, b):
    return jnp.dot(a, b, preferred_element_type=F32)


def _dot_nt(a, b):
    return lax.dot_general(a, b, (((1,), (1,)), ((), ())), preferred_element_type=F32)


def _dot_tn(a, b):
    return lax.dot_general(a, b, (((0,), (0,)), ((), ())), preferred_element_type=F32)


def _norm_matmul_kernel(x_ref, g_ref, w_ref, o_ref):
    h = _rms(x_ref[...], g_ref[...]).astype(BF16)
    o_ref[...] = _dot(h, w_ref[...]).astype(o_ref.dtype)


def _norm_matmul(x2, g, w, tm):
    t, d = x2.shape
    n = w.shape[1]
    return pl.pallas_call(
        _norm_matmul_kernel,
        grid=(t // tm,),
        in_specs=[pl.BlockSpec((tm, d), lambda i: (i, 0)),
                  pl.BlockSpec((1, d), lambda i: (0, 0)),
                  pl.BlockSpec((d, n), lambda i: (0, 0))],
        out_specs=pl.BlockSpec((tm, n), lambda i: (i, 0)),
        out_shape=jax.ShapeDtypeStruct((t, n), F32),
        compiler_params=_cparams(("parallel",)),
        name="even_in_proj",
    )(x2, g, w)


def _even_prep_kernel(aprev_ref, z_ref, anext_ref, pw_ref, ps_ref, qn_ref, wqm_ref, wqr_ref, cq_ref, sq_ref,
                      kvn_ref, wk_ref, wvt_ref, esel_ref, cos_ref, sin_ref,
                      pool_ref, qt_ref, k_ref, vt_ref, *, tm, seq_len):
    i = pl.program_id(1)
    nt = pl.num_programs(1)
    z = z_ref[0]
    a = z[:, :D_POOL]
    prev = jnp.where(i == 0, 0.0, aprev_ref[0])
    nxt = jnp.where(i == nt - 1, 0.0, anext_ref[0])
    ext = jnp.concatenate([prev, a, nxt], axis=0)
    n_ext = tm + 2 * HALO
    pos = i * tm + lax.broadcasted_iota(jnp.int32, (tm, 1), 0)
    outs = []
    for gi, w in enumerate(POOL_WINDOWS):
        lo, hi = gi * POOL_GROUP, (gi + 1) * POOL_GROUP
        before = w // 2
        after = w - 1 - before
        p = ext[:, lo:hi]
        span = 1
        while span < w:
            p = p + pltpu.roll(p, span, 0)
            span *= 2
        if after:
            p = pltpu.roll(p, n_ext - after, 0)
        s = p[HALO:HALO + tm]
        cnt = (jnp.minimum(pos + after, seq_len - 1) - jnp.maximum(pos - before, 0) + 1).astype(F32)
        r = s / cnt - a[:, lo:hi]
        outs.append(_dot(r.astype(BF16), pw_ref[gi]))
    pool = jnp.concatenate(outs, axis=-1) * ps_ref[...]
    pool_ref[0] = pool.astype(pool_ref.dtype)

    cq = z[:, D_POOL:D_POOL + Q_LORA]
    ckv = z[:, D_POOL + Q_LORA:D_POOL + Q_LORA + KV_LORA]
    kr = z[:, 896:928]
    krr = z[:, 928:960]
    cqn = _rms(cq, qn_ref[...]).astype(BF16)
    qt = _dot_nt(wqm_ref[...], cqn).reshape(MLA_HEADS, HEAD_PAD, tm)
    qtr = _dot_nt(wqr_ref[...], cqn).reshape(MLA_HEADS, HEAD_PAD, tm)
    qt = qt * cq_ref[...][None] + qtr * sq_ref[...][None]
    qt_ref[0] = qt.reshape(MLA_HEADS * HEAD_PAD, tm).astype(qt_ref.dtype)

    kvn = _rms(ckv, kvn_ref[...]).astype(BF16)
    k_rope = kr * cos_ref[...] + krr * sin_ref[...]
    kall = _dot(kvn, wk_ref[...]) + _dot(k_rope.astype(BF16), esel_ref[...])
    for h in range(MLA_HEADS):
        k_ref[0, h] = kall[:, h * HEAD_PAD:(h + 1) * HEAD_PAD].astype(k_ref.dtype)
    vt_ref[0, 0] = _dot_nt(wvt_ref[...], kvn).astype(vt_ref.dtype)


def _even_prep(z3, wts, tabs, tm):
    b, seq_len, _ = z3.shape
    nt = seq_len // tm
    hb = tm // HALO
    nhb = seq_len // HALO
    full = lambda shape: pl.BlockSpec(shape, lambda bi, i: (0,) * len(shape))
    in_specs = [
        pl.BlockSpec((1, HALO, D_POOL), lambda bi, i: (bi, jnp.maximum(i * hb - 1, 0), 0)),
        pl.BlockSpec((1, tm, D_IN_EVEN_PAD), lambda bi, i: (bi, i, 0)),
        pl.BlockSpec((1, HALO, D_POOL), lambda bi, i: (bi, jnp.minimum((i + 1) * hb, nhb - 1), 0)),
        full((4, POOL_GROUP, POOL_GROUP)), full((1, D_POOL)), full((1, Q_LORA)),
        full((MLA_HEADS * HEAD_PAD, Q_LORA)), full((MLA_HEADS * HEAD_PAD, Q_LORA)),
        pl.BlockSpec((HEAD_PAD, tm), lambda bi, i: (0, i)),
        pl.BlockSpec((HEAD_PAD, tm), lambda bi, i: (0, i)),
        full((1, KV_LORA)), full((KV_LORA, MLA_HEADS * HEAD_PAD)), full((MLA_HEADS * V_HEAD, KV_LORA)),
        full((QK_ROPE, MLA_HEADS * HEAD_PAD)),
        pl.BlockSpec((tm, QK_ROPE), lambda bi, i: (i, 0)),
        pl.BlockSpec((tm, QK_ROPE), lambda bi, i: (i, 0)),
    ]
    out_specs = [
        pl.BlockSpec((1, tm, D_POOL), lambda bi, i: (bi, i, 0)),
        pl.BlockSpec((1, MLA_HEADS * HEAD_PAD, tm), lambda bi, i: (bi, 0, i)),
        pl.BlockSpec((1, MLA_HEADS, tm, HEAD_PAD), lambda bi, i: (bi, 0, i, 0)),
        pl.BlockSpec((1, 1, MLA_HEADS * V_HEAD, tm), lambda bi, i: (bi, i, 0, 0)),
    ]
    out_shape = [
        jax.ShapeDtypeStruct((b, seq_len, D_POOL), BF16),
        jax.ShapeDtypeStruct((b, MLA_HEADS * HEAD_PAD, seq_len), BF16),
        jax.ShapeDtypeStruct((b, MLA_HEADS, seq_len, HEAD_PAD), BF16),
        jax.ShapeDtypeStruct((b, nt, MLA_HEADS * V_HEAD, tm), BF16),
    ]
    return pl.pallas_call(
        functools.partial(_even_prep_kernel, tm=tm, seq_len=seq_len),
        grid=(b, nt), in_specs=in_specs, out_specs=out_specs, out_shape=out_shape,
        compiler_params=_cparams(("parallel", "parallel")),
        name="even_prep",
    )(z3, z3, z3, wts["pool_w"], wts["pool_scale"], wts["q_norm"], wts["wq_main_t"], wts["wq_rot_t"],
      tabs["cq"], tabs["sq"], wts["kv_norm"], wts["w_k"], wts["w_vt"], wts["e_sel"], tabs["cos2"], tabs["sin2"])


def _attn_kernel(qt_ref, k_ref, vt_ref, o_ref, *, kc, nkc):
    qt = qt_ref[0]
    tq = qt.shape[1]

    def body(j, carry):
        m, l, acc = carry
        off = pl.multiple_of(j * kc, kc)
        k = k_ref[0, 0, pl.ds(off, kc), :]
        s = _dot(k, qt)
        m_new = jnp.maximum(m, jnp.max(s, axis=0, keepdims=True))
        alpha = jnp.exp2(m - m_new)
        p = jnp.exp2(s - m_new)
        l = alpha * l + jnp.sum(p, axis=0, keepdims=True)
        acc = alpha * acc + _dot(vt_ref[0, j], p.astype(BF16))
        return m_new, l, acc

    init = (jnp.full((1, tq), -jnp.inf, F32), jnp.zeros((1, tq), F32), jnp.zeros((V_HEAD, tq), F32))
    _, l, acc = lax.fori_loop(0, nkc, body, init)
    o_ref[0] = (acc / l).astype(o_ref.dtype)


def _attention(qt, k, vt, tq, kc):
    b, _, seq_len = qt.shape
    nkc = seq_len // kc
    return pl.pallas_call(
        functools.partial(_attn_kernel, kc=kc, nkc=nkc),
        grid=(b, MLA_HEADS, seq_len // tq),
        in_specs=[pl.BlockSpec((1, HEAD_PAD, tq), lambda bi, h, qi: (bi, h, qi)),
                  pl.BlockSpec((1, 1, seq_len, HEAD_PAD), lambda bi, h, qi: (bi, h, 0, 0)),
                  pl.BlockSpec((1, nkc, V_HEAD, kc), lambda bi, h, qi: (bi, 0, h, 0))],
        out_specs=pl.BlockSpec((1, V_HEAD, tq), lambda bi, h, qi: (bi, h, qi)),
        out_shape=jax.ShapeDtypeStruct((b, MLA_HEADS * V_HEAD, seq_len), BF16),
        compiler_params=_cparams(("parallel", "parallel", "parallel")),
        name="mla_attention",
    )(qt, k, vt)


def _outproj_kernel(x_ref, a_ref, b_ref, wa_ref, wb_ref, g_ref, o_ref, *, b_transposed):
    m = _dot(a_ref[0], wa_ref[...])
    if b_transposed:
        m = m + _dot_tn(b_ref[0], wb_ref[...])
    else:
        m = m + _dot(b_ref[0], wb_ref[...])
    o_ref[0] = x_ref[0] + _rms(m, g_ref[...])


def _outproj(x3, a3, b3, wa, wb, g, tm, b_transposed):
    b, seq_len, d = x3.shape
    da, db = wa.shape[0], wb.shape[0]
    b_spec = (pl.BlockSpec((1, db, tm), lambda bi, i: (bi, 0, i)) if b_transposed
              else pl.BlockSpec((1, tm, db), lambda bi, i: (bi, i, 0)))
    return pl.pallas_call(
        functools.partial(_outproj_kernel, b_transposed=b_transposed),
        grid=(b, seq_len // tm),
        in_specs=[pl.BlockSpec((1, tm, d), lambda bi, i: (bi, i, 0)),
                  pl.BlockSpec((1, tm, da), lambda bi, i: (bi, i, 0)),
                  b_spec,
                  pl.BlockSpec((da, d), lambda bi, i: (0, 0)),
                  pl.BlockSpec((db, d), lambda bi, i: (0, 0)),
                  pl.BlockSpec((1, d), lambda bi, i: (0, 0))],
        out_specs=pl.BlockSpec((1, tm, d), lambda bi, i: (bi, i, 0)),
        out_shape=jax.ShapeDtypeStruct((b, seq_len, d), F32),
        compiler_params=_cparams(("parallel", "parallel")),
        name="mixer_out_proj",
    )(x3, a3, b3, wa, wb, g)


def _mlp_kernel(x_ref, g1_ref, w1_ref, w2_ref, g2_ref, o_ref, *, fc):
    x = x_ref[...]
    h = _rms(x, g1_ref[...]).astype(BF16)
    acc = jnp.zeros(x.shape, F32)
    for c in range(D_FF // fc):
        f = _dot(h, w1_ref[:, c * fc:(c + 1) * fc])
        f = jnp.square(jnp.maximum(f, 0.0)).astype(BF16)
        acc = acc + _dot(f, w2_ref[c * fc:(c + 1) * fc, :])
    o_ref[...] = x + _rms(acc, g2_ref[...])


def _mlp(x2, g1, w1, w2, g2, tm):
    t, d = x2.shape
    return pl.pallas_call(
        functools.partial(_mlp_kernel, fc=1024),
        grid=(t // tm,),
        in_specs=[pl.BlockSpec((tm, d), lambda i: (i, 0)),
                  pl.BlockSpec((1, d), lambda i: (0, 0)),
                  pl.BlockSpec((d, D_FF), lambda i: (0, 0)),
                  pl.BlockSpec((D_FF, d), lambda i: (0, 0)),
                  pl.BlockSpec((1, d), lambda i: (0, 0))],
        out_specs=pl.BlockSpec((tm, d), lambda i: (i, 0)),
        out_shape=jax.ShapeDtypeStruct((t, d), F32),
        compiler_params=_cparams(("parallel",)),
        name="relu2_mlp",
    )(x2, g1, w1, w2, g2)


def _odd_in_kernel(xprev_ref, x_ref, xnext_ref, g_ref, w_ref, sw_ref, sb_ref, dm_ref,
                   v_ref, x1_ref, x2_ref, wr_ref, wi_ref, *, tm):
    i = pl.program_id(1)
    nt = pl.num_programs(1)
    prev = jnp.where(i == 0, 0.0, xprev_ref[0])
    nxt = jnp.where(i == nt - 1, 0.0, xnext_ref[0])
    xe = jnp.concatenate([prev, x_ref[0], nxt], axis=0)
    n_ext = tm + 2 * HALO
    h = _rms(xe, g_ref[...]).astype(BF16)
    z = _dot(h, w_ref[...])
    zh = z[:, :D_ZH]
    up = pltpu.roll(zh, 1, 0)[HALO:HALO + tm]
    dn = pltpu.roll(zh, n_ext - 1, 0)[HALO:HALO + tm]
    mid = zh[HALO:HALO + tm]
    sw = sw_ref[...]
    zc = up * sw[0:1] + mid * sw[1:2] + dn * sw[2:3] + sb_ref[...]
    v_ref[0] = zc[:, :D_HYENA].astype(v_ref.dtype)
    x1_ref[0] = zc[:, D_HYENA:2 * D_HYENA].astype(x1_ref.dtype)
    x2_ref[0] = zc[:, 2 * D_HYENA:].astype(x2_ref.dtype)
    f = z[HALO:HALO + tm, D_ZH:].astype(BF16)
    gf = _dot(f, dm_ref[...])
    wr_ref[0] = gf[:, :D_FNET].astype(wr_ref.dtype)
    wi_ref[0] = gf[:, D_FNET:].astype(wi_ref.dtype)


def _odd_in(x3, g, w, sw, sb, dmat, tm):
    b, seq_len, d = x3.shape
    hb = tm // HALO
    nhb = seq_len // HALO
    full = lambda shape: pl.BlockSpec(shape, lambda bi, i: (0,) * len(shape))
    tok = lambda c: pl.BlockSpec((1, tm, c), lambda bi, i: (bi, i, 0))
    return pl.pallas_call(
        functools.partial(_odd_in_kernel, tm=tm),
        grid=(b, seq_len // tm),
        in_specs=[pl.BlockSpec((1, HALO, d), lambda bi, i: (bi, jnp.maximum(i * hb - 1, 0), 0)),
                  tok(d),
                  pl.BlockSpec((1, HALO, d), lambda bi, i: (bi, jnp.minimum((i + 1) * hb, nhb - 1), 0)),
                  full((1, d)), full((d, D_IN_ODD)), full((3, D_ZH)), full((1, D_ZH)), full((D_FNET, 2 * D_FNET))],
        out_specs=[tok(D_HYENA), tok(D_HYENA), tok(D_HYENA), tok(D_FNET), tok(D_FNET)],
        out_shape=[jax.ShapeDtypeStruct((b, seq_len, D_HYENA), BF16)] * 3
                  + [jax.ShapeDtypeStruct((b, seq_len, D_FNET), BF16)] * 2,
        compiler_params=_cparams(("parallel", "parallel")),
        name="odd_in_proj_conv",
    )(x3, x3, x3, g, w, sw, sb, dmat)


def _filter_kernel(feat_ref, w1_ref, b1_ref, w2_ref, b2_ref, w3_ref, fr_ref, dl_ref, hf_ref, ss_ref, *, tl):
    i = pl.program_id(0)
    hp = lax.Precision.HIGHEST
    feat = feat_ref[...]
    fr = fr_ref[...]
    hid = jnp.sin(fr * (jnp.dot(feat, w1_ref[...], precision=hp, preferred_element_type=F32) + b1_ref[...]))
    hid = jnp.sin(fr * (jnp.dot(hid, w2_ref[...], precision=hp, preferred_element_type=F32) + b2_ref[...]))
    hf = jnp.dot(hid, w3_ref[...], precision=hp, preferred_element_type=F32)
    dec = jnp.exp(-feat[:, 0:1] * dl_ref[...])
    hf = hf * jnp.concatenate([dec] * (2 * HYENA_ORDER), axis=-1)
    part = jnp.sum(hf * hf, axis=0, keepdims=True)

    @pl.when(i == 0)
    def _():
        ss_ref[...] = jnp.zeros_like(ss_ref)

    ss_ref[...] += part
    row = i * tl + lax.broadcasted_iota(jnp.int32, hf.shape, 0)
    col = lax.broadcasted_iota(jnp.int32, hf.shape, 1)
    is_bwd = jnp.logical_or(jnp.logical_and(col >= D_HYENA, col < 2 * D_HYENA), col >= 3 * D_HYENA)
    hf_ref[...] = jnp.where(jnp.logical_and(row == 0, is_bwd), 0.0, hf).astype(hf_ref.dtype)


def _filters(feat, w1, b1, w2, b2, w3, fr, deltas, tl):
    seq_len = feat.shape[0]
    nf = 2 * HYENA_ORDER * D_HYENA
    full = lambda shape: pl.BlockSpec(shape, lambda i: (0,) * len(shape))
    return pl.pallas_call(
        functools.partial(_filter_kernel, tl=tl),
        grid=(seq_len // tl,),
        in_specs=[pl.BlockSpec((tl, FEAT_PAD), lambda i: (i, 0)),
                  full((FEAT_PAD, FILTER_HIDDEN)), full((1, FILTER_HIDDEN)),
                  full((FILTER_HIDDEN, FILTER_HIDDEN)), full((1, FILTER_HIDDEN)),
                  full((FILTER_HIDDEN, nf)), full((1, FILTER_HIDDEN)), full((1, D_HYENA))],
        out_specs=[pl.BlockSpec((tl, nf), lambda i: (i, 0)), full((1, nf))],
        out_shape=[jax.ShapeDtypeStruct((seq_len, nf), BF16), jax.ShapeDtypeStruct((1, nf), F32)],
        compiler_params=_cparams(("arbitrary",)),
        name="hyena_filters",
    )(feat, w1, b1, w2, b2, w3, fr, deltas)


def _outer_kernel(f_ref, zr_ref, zi_ref, o_ref):
    z = jnp.concatenate([zr_ref[0], zi_ref[0]], axis=0)
    o_ref[0] = _dot(f_ref[...], z).astype(o_ref.dtype)


def _outer_real_kernel(f_ref, zr_ref, o_ref):
    o_ref[0] = _dot(f_ref[...], zr_ref[0]).astype(o_ref.dtype)


def _outer_stage(fmat, zr, zi, g_count, zr_map, zi_map, lc, name):
    _, r, lanes = zr.shape
    mo = fmat.shape[0]
    in_specs = [pl.BlockSpec(fmat.shape, lambda g, j: (0, 0)),
                pl.BlockSpec((1, r, lc), lambda g, j: (zr_map(g), 0, j))]
    args = [fmat, zr]
    kern = _outer_real_kernel
    if zi is not None:
        in_specs.append(pl.BlockSpec((1, r, lc), lambda g, j: (zi_map(g), 0, j)))
        args.append(zi)
        kern = _outer_kernel
    return pl.pallas_call(
        kern, grid=(g_count, lanes // lc), in_specs=in_specs,
        out_specs=pl.BlockSpec((1, mo, lc), lambda g, j: (g, 0, j)),
        out_shape=jax.ShapeDtypeStruct((g_count, mo, lanes), BF16),
        compiler_params=_cparams(("parallel", "parallel")),
        name=name,
    )(*args)


def _slab_mm(m, ar, ai):
    return _dot(m[:, :FFT_N2], ar) + _dot(m[:, FFT_N2:], ai)


def _hyena_mid_kernel(a_ref, mf_ref, mi_ref, kf_ref, b_ref, *, k1c):
    for t in range(k1c):
        rows = slice(t * FFT_N2, (t + 1) * FFT_N2)
        x = _slab_mm(mf_ref[t], a_ref[0, 0, rows, :], a_ref[0, 1, rows, :])
        xr, xi = x[:FFT_N2], x[FFT_N2:]
        kr, ki = kf_ref[0, rows, :], kf_ref[1, rows, :]
        yr = (xr * kr - xi * ki).astype(BF16)
        yi = (xr * ki + xi * kr).astype(BF16)
        bq = _slab_mm(mi_ref[t], yr, yi)
        b_ref[0, 0, rows, :] = bq[:FFT_N2].astype(b_ref.dtype)
        b_ref[0, 1, rows, :] = bq[FFT_N2:].astype(b_ref.dtype)


def _hyena_mid(a4, mf, mi, kf, order, k1c):
    g_count, _, n, c = a4.shape
    n1 = n // FFT_N2
    rows = k1c * FFT_N2
    return pl.pallas_call(
        functools.partial(_hyena_mid_kernel, k1c=k1c),
        grid=(n1 // k1c, g_count),
        in_specs=[pl.BlockSpec((1, 2, rows, c), lambda j, g: (g, 0, j, 0)),
                  pl.BlockSpec((k1c, 2 * FFT_N2, 2 * FFT_N2), lambda j, g: (j, 0, 0)),
                  pl.BlockSpec((k1c, 2 * FFT_N2, 2 * FFT_N2), lambda j, g: (j, 0, 0)),
                  pl.BlockSpec((None, 2, rows, c), lambda j, g: (order, 0, j, 0))],
        out_specs=pl.BlockSpec((1, 2, rows, c), lambda j, g: (g, 0, j, 0)),
        out_shape=jax.ShapeDtypeStruct(a4.shape, BF16),
        compiler_params=_cparams(("parallel", "parallel")),
        name="hyena_mid",
    )(a4, mf, mi, kf)


def _filter_spec_kernel(af_ref, ab_ref, mf_ref, ss_ref, kf_ref, *, k1c):
    sc = lax.rsqrt(ss_ref[...] + EPS)
    sf, sb = sc[0:1], sc[1:2]
    for t in range(k1c):
        rows = slice(t * FFT_N2, (t + 1) * FFT_N2)
        xf = _slab_mm(mf_ref[t], af_ref[0, 0, rows, :], af_ref[0, 1, rows, :])
        xb = _slab_mm(mf_ref[t], ab_ref[0, 0, rows, :], ab_ref[0, 1, rows, :])
        kf_ref[0, 0, rows, :] = sf * xf[:FFT_N2] + sb * xb[:FFT_N2]
        kf_ref[0, 1, rows, :] = sf * xf[FFT_N2:] - sb * xb[FFT_N2:]


def _filter_spectra(a4, mf, ss, k1c):
    _, _, n, c = a4.shape
    n1 = n // FFT_N2
    rows = k1c * FFT_N2
    return pl.pallas_call(
        functools.partial(_filter_spec_kernel, k1c=k1c),
        grid=(HYENA_ORDER, n1 // k1c),
        in_specs=[pl.BlockSpec((1, 2, rows, c), lambda o, j: (2 * o, 0, j, 0)),
                  pl.BlockSpec((1, 2, rows, c), lambda o, j: (2 * o + 1, 0, j, 0)),
                  pl.BlockSpec((k1c, 2 * FFT_N2, 2 * FFT_N2), lambda o, j: (j, 0, 0)),
                  pl.BlockSpec((None, 2, c), lambda o, j: (o, 0, 0))],
        out_specs=pl.BlockSpec((1, 2, rows, c), lambda o, j: (o, 0, j, 0)),
        out_shape=jax.ShapeDtypeStruct((HYENA_ORDER, 2, n, c), F32),
        compiler_params=_cparams(("parallel", "parallel")),
        name="hyena_filter_spectra",
    )(a4, a4, mf, ss)


def _hyena_gate_kernel(g1_ref, b_ref, ua_ref, ub_ref, ga_ref, gb_ref, bias_ref, o_ref):
    y = _dot(g1_ref[...], b_ref[0])
    n1h = y.shape[0] // 2
    bias = bias_ref[...]
    ya = y[:n1h] + ua_ref[0].astype(F32) * bias
    yb = y[n1h:] + ub_ref[0].astype(F32) * bias
    o_ref[0, 0] = (ga_ref[0].astype(F32) * ya).astype(o_ref.dtype)
    o_ref[0, 1] = (gb_ref[0].astype(F32) * yb).astype(o_ref.dtype)


def _hyena_gate(g1, b_outer, u_in, gate, bias_t, lc):
    bsz, n1h, lanes = u_in.shape
    pairs = bsz // 2
    ev = lambda: pl.BlockSpec((1, n1h, lc), lambda p, j: (2 * p, 0, j))
    od = lambda: pl.BlockSpec((1, n1h, lc), lambda p, j: (2 * p + 1, 0, j))
    out = pl.pallas_call(
        _hyena_gate_kernel,
        grid=(pairs, lanes // lc),
        in_specs=[pl.BlockSpec(g1.shape, lambda p, j: (0, 0)),
                  pl.BlockSpec((1, b_outer.shape[1], lc), lambda p, j: (p, 0, j)),
                  ev(), od(), ev(), od(),
                  pl.BlockSpec((1, lc), lambda p, j: (0, 0))],
        out_specs=pl.BlockSpec((1, 2, n1h, lc), lambda p, j: (p, 0, 0, j)),
        out_shape=jax.ShapeDtypeStruct((pairs, 2, n1h, lanes), BF16),
        compiler_params=_cparams(("parallel", "parallel")),
        name="hyena_gate",
    )(g1, b_outer, u_in, u_in, gate, gate, bias_t)
    return out.reshape(bsz, n1h, lanes)


def _fnet_mid_kernel(a_ref, m_ref, fw_ref, o_ref, *, k1c):
    for t in range(k1c):
        rows = slice(t * FFT_N2, (t + 1) * FFT_N2)
        x = _slab_mm(m_ref[t], a_ref[0, 0, rows, :], a_ref[0, 1, rows, :])
        o_ref[0, rows, :] = _dot(x.astype(BF16), fw_ref[...]).astype(o_ref.dtype)


def _fnet_mid(a4, mre, fw, k1c):
    bsz, _, n, c = a4.shape
    n1 = n // FFT_N2
    rows = k1c * FFT_N2
    return pl.pallas_call(
        functools.partial(_fnet_mid_kernel, k1c=k1c),
        grid=(n1 // k1c, bsz),
        in_specs=[pl.BlockSpec((1, 2, rows, c), lambda j, g: (g, 0, j, 0)),
                  pl.BlockSpec((k1c, FFT_N2, 2 * FFT_N2), lambda j, g: (j, 0, 0)),
                  pl.BlockSpec((c, c), lambda j, g: (0, 0))],
        out_specs=pl.BlockSpec((1, rows, c), lambda j, g: (g, j, 0)),
        out_shape=jax.ShapeDtypeStruct((bsz, n, c), BF16),
        compiler_params=_cparams(("parallel", "parallel")),
        name="fnet_mid",
    )(a4, mre, fw)


def _cis_table(num, den):
    ang = (2.0 * np.pi / den) * (num % den).astype(np.float64)
    return np.cos(ang), np.sin(ang)


def _block(cr, ci):
    return np.block([[cr, -ci], [ci, cr]])


def _outer_matrices(n1, n_in, inverse_scale=None):
    k = np.arange(n1)[:, None] * np.arange(n_in)[None, :]
    c, s = _cis_table(k, n1)
    if inverse_scale is None:
        return _block(c, -s)
    return _block(c.T * inverse_scale, s.T * inverse_scale)


def _slab_matrices(n1):
    n = n1 * FFT_N2
    k1 = jnp.arange(n1, dtype=jnp.int32)[:, None, None]
    k2 = jnp.arange(FFT_N2, dtype=jnp.int32)[None, :, None]
    n2 = jnp.arange(FFT_N2, dtype=jnp.int32)[None, None, :]
    num = (n2 * k2 * n1 + n2 * k1) % n
    ang = num.astype(F32) * (2.0 * math.pi / n)
    c, s = jnp.cos(ang), -jnp.sin(ang)
    top = jnp.concatenate([c, -s], axis=2)
    bot = jnp.concatenate([s, c], axis=2)
    fwd = jnp.concatenate([top, bot], axis=1)
    inv = jnp.swapaxes(fwd, 1, 2)
    return fwd, inv


def _rope_tables(seq_len):
    inv = ROPE_THETA ** (-jnp.arange(0, QK_ROPE, 2, dtype=F32) / QK_ROPE)
    ang = jnp.arange(seq_len, dtype=F32)[:, None] * inv[None, :]
    cos, sin = jnp.cos(ang), jnp.sin(ang)
    cos2 = jnp.concatenate([cos, cos], axis=-1)
    sin2 = jnp.concatenate([sin, sin], axis=-1)
    c = (QK_NOPE + QK_ROPE) ** -0.5 * math.log2(math.e)
    pad = HEAD_PAD - QK_NOPE - QK_ROPE
    cq = jnp.concatenate([jnp.full((QK_NOPE, seq_len), c, F32), c * cos2.T, jnp.zeros((pad, seq_len), F32)], axis=0)
    sq = jnp.concatenate([jnp.zeros((QK_NOPE, seq_len), F32), c * sin2.T, jnp.zeros((pad, seq_len), F32)], axis=0)
    return {"cos2": cos2, "sin2": sin2, "cq": cq, "sq": sq}


def _filter_features(seq_len):
    t = jnp.linspace(0.0, 1.0, seq_len, dtype=F32)
    bands = jnp.linspace(1e-4, POS_BANDS - 1, POS_BANDS, dtype=F32)
    ang = (2.0 * math.pi / seq_len) * jnp.arange(seq_len, dtype=F32)[:, None] * bands[None, :]
    feat = jnp.concatenate([t[:, None], jnp.cos(ang), -jnp.sin(ang)], axis=-1)
    return jnp.pad(feat, ((0, 0), (0, FEAT_PAD - POS_EMB)))


def _rot_cols(w):
    half = QK_ROPE // 2
    return jnp.concatenate([-w[..., half:], w[..., :half]], axis=-1)


def _prep_even_weights(even_w_in, pool_w, pool_scale, q_norm, w_uq, kv_norm, w_ukv, even_w_out):
    d_used = D_POOL + Q_LORA + KV_LORA + QK_ROPE
    w_rope = even_w_in[:, d_used - QK_ROPE:d_used]
    w_in = jnp.concatenate([even_w_in, _rot_cols(w_rope),
                            jnp.zeros((D_MODEL, D_IN_EVEN_PAD - d_used - QK_ROPE), F32)], axis=1).astype(BF16)
    wq = w_uq.reshape(Q_LORA, MLA_HEADS, QK_NOPE + QK_ROPE)
    pad = jnp.zeros((Q_LORA, MLA_HEADS, HEAD_PAD - QK_NOPE - QK_ROPE), F32)
    wq_main = jnp.concatenate([wq, pad], axis=-1).reshape(Q_LORA, MLA_HEADS * HEAD_PAD)
    wq_rot = jnp.concatenate([jnp.zeros((Q_LORA, MLA_HEADS, QK_NOPE), F32), _rot_cols(wq[..., QK_NOPE:]), pad],
                             axis=-1).reshape(Q_LORA, MLA_HEADS * HEAD_PAD)
    wkv = w_ukv.reshape(KV_LORA, MLA_HEADS, QK_NOPE + V_HEAD)
    w_k = jnp.concatenate([wkv[..., :QK_NOPE], jnp.zeros((KV_LORA, MLA_HEADS, HEAD_PAD - QK_NOPE), F32)],
                          axis=-1).reshape(KV_LORA, MLA_HEADS * HEAD_PAD)
    w_v = wkv[..., QK_NOPE:].reshape(KV_LORA, MLA_HEADS * V_HEAD)
    e = np.zeros((QK_ROPE, MLA_HEADS, HEAD_PAD), np.float32)
    for j in range(QK_ROPE):
        e[j, :, QK_NOPE + j] = 1.0
    return {
        "w_in": w_in,
        "pool_w": pool_w.astype(BF16),
        "pool_scale": pool_scale.reshape(1, D_POOL),
        "q_norm": q_norm.reshape(1, Q_LORA),
        "wq_main_t": wq_main.T.astype(BF16),
        "wq_rot_t": wq_rot.T.astype(BF16),
        "kv_norm": kv_norm.reshape(1, KV_LORA),
        "w_k": w_k.astype(BF16),
        "w_vt": w_v.T.astype(BF16),
        "e_sel": jnp.asarray(e.reshape(QK_ROPE, MLA_HEADS * HEAD_PAD), BF16),
        "w_out_a": even_w_out[:D_POOL].astype(BF16),
        "w_out_b": even_w_out[D_POOL:].astype(BF16),
    }


def _fnet_channel_dft():
    k = np.arange(FNET_GROUP)[:, None] * np.arange(FNET_GROUP)[None, :]
    c, s = _cis_table(k, FNET_GROUP)
    eye = np.eye(FNET_GROUPS)
    return np.concatenate([np.kron(eye, c), np.kron(eye, -s)], axis=1)


def _to_outer(x3, n_rows):
    g, _, c = x3.shape
    return x3.reshape(g, n_rows, FFT_N2 * c)


def _even_layer(x, wts, g_mix, seq_tabs):
    b, seq_len, d = x.shape
    tm = min(TM, seq_len)
    z = _norm_matmul(x.reshape(b * seq_len, d), g_mix[0].reshape(1, d), wts["w_in"], tm)
    pool, qt, k, vt = _even_prep(z.reshape(b, seq_len, D_IN_EVEN_PAD), wts, seq_tabs, tm)
    ot = _attention(qt, k, vt, min(TQ, seq_len), tm)
    return _outproj(x, pool, ot, wts["w_out_a"], wts["w_out_b"], g_mix[1].reshape(1, d), tm, True)


def _hyena_fnet(v, x1, x2, wr, wi, ow, consts):
    b, seq_len, _ = v.shape
    n1 = 2 * seq_len // FFT_N2
    n1h = n1 // 2
    n = n1 * FFT_N2
    lanes = FFT_N2 * D_HYENA
    lc = LANE_CHUNK
    pairs = b // 2

    hf, ss = _filters(consts["feat"], ow["filt_w1"], ow["filt_b1"], ow["filt_w2"], ow["filt_b2"], ow["filt_w3"],
                      ow["filt_freq"], consts["deltas"], min(256, seq_len))
    hf = hf.reshape(seq_len, 2 * HYENA_ORDER, D_HYENA).transpose(1, 0, 2)
    fa = _outer_stage(consts["hy_f1"][:, :n1h], _to_outer(hf, n1h), None, 2 * HYENA_ORDER,
                      lambda g: g, None, lc, "hyena_filter_outer")
    kf = _filter_spectra(fa.reshape(2 * HYENA_ORDER, 2, n, D_HYENA), consts["hy_mf"],
                         ss.reshape(HYENA_ORDER, 2, D_HYENA), K1C)

    u = _to_outer(v, n1h)
    gates = (_to_outer(x1, n1h), _to_outer(x2, n1h))
    for o in range(HYENA_ORDER):
        a = _outer_stage(consts["hy_f1"], u, u, pairs, lambda g: 2 * g, lambda g: 2 * g + 1, lc, "hyena_outer")
        bq = _hyena_mid(a.reshape(pairs, 2, n, D_HYENA), consts["hy_mf"], consts["hy_mi"], kf, o, K1C)
        bias_t = jnp.tile(ow["hyena_bias"][o].reshape(1, D_HYENA), (1, lc // D_HYENA))
        u = _hyena_gate(consts["hy_g1"], bq.reshape(pairs, 2 * n1, lanes), u, gates[o], bias_t, lc)
    hy = u.reshape(b, seq_len, D_HYENA)

    f1 = seq_len // FFT_N2
    fl = FFT_N2 * D_FNET
    a = _outer_stage(consts["fn_f1"], _to_outer(wr, f1), _to_outer(wi, f1), b, lambda g: g, lambda g: g,
                     min(4096, fl), "fnet_outer")
    fm = _fnet_mid(a.reshape(b, 2, seq_len, D_FNET), consts["fn_mre"], ow["fnet_w"], min(K1C, f1))
    fm = fm.reshape(b, f1, FFT_N2, D_FNET).transpose(0, 2, 1, 3).reshape(b, seq_len, D_FNET)
    return hy, fm


def _odd_layer(x, ow, g_mix, consts):
    b, seq_len, d = x.shape
    tm = min(TM, seq_len)
    v, x1, x2, wr, wi = _odd_in(x, g_mix[0].reshape(1, d), ow["w_in"], ow["short_w"], ow["short_b"], ow["dmat"], tm)
    hy, fm = _hyena_fnet(v, x1, x2, wr, wi, ow, consts)
    return _outproj(x, hy, fm, ow["w_out_a"], ow["w_out_b"], g_mix[1].reshape(1, d), tm, False)


def _seq_constants(seq_len):
    n1 = 2 * seq_len // FFT_N2
    n1h = n1 // 2
    n = n1 * FFT_N2
    f1 = seq_len // FFT_N2
    hy_mf, hy_mi = _slab_matrices(n1)
    fn_mf, _ = _slab_matrices(f1)
    deltas = jnp.abs(jnp.linspace(math.log(DECAY_TARGET) / SLOW_DECAY_PCT, math.log(DECAY_TARGET) / FAST_DECAY_PCT,
                                  D_HYENA, dtype=F32)).reshape(1, D_HYENA)
    return {
        "rope": _rope_tables(seq_len),
        "feat": _filter_features(seq_len),
        "deltas": deltas,
        "hy_f1": jnp.asarray(_outer_matrices(n1, n1h), BF16),
        "hy_g1": jnp.asarray(_outer_matrices(n1, n1h, inverse_scale=1.0 / n), BF16),
        "hy_mf": hy_mf.astype(BF16),
        "hy_mi": hy_mi.astype(BF16),
        "fn_f1": jnp.asarray(_outer_matrices(f1, f1), BF16),
        "fn_mre": (fn_mf[:, :FFT_N2, :] * (1.0 / math.sqrt(seq_len * FNET_GROUP))).astype(BF16),
    }


def _trunk(x, norm_mix, norm_mlp, ffw, even_wts, odd_wts):
    b, seq_len, d = x.shape
    consts = _seq_constants(seq_len)
    tm = min(TM, seq_len)
    for layer in range(2):
        if layer == 0:
            x = _even_layer(x, even_wts, norm_mix[0], consts["rope"])
        else:
            x = _odd_layer(x, odd_wts, norm_mix[1], consts)
        w1, w2 = ffw[layer]
        x = _mlp(x.reshape(b * seq_len, d), norm_mlp[layer, 0].reshape(1, d), w1, w2,
                 norm_mlp[layer, 1].reshape(1, d), tm).reshape(b, seq_len, d)
    return x


def kernel(x_prompt, x_sample, norm_mix, norm_mlp, w_ff_in, w_ff_out, even_w_in, pool_w, pool_scale, mla_q_norm,
           mla_w_uq, mla_kv_norm, mla_w_ukv, even_w_out, odd_w_in, short_w, short_b, filt_w1, filt_b1, filt_w2,
           filt_b2, filt_w3, filt_freq, hyena_bias, fnet_w, odd_w_out):
    even_wts = _prep_even_weights(even_w_in[0], pool_w[0], pool_scale[0], mla_q_norm[0], mla_w_uq[0],
                                  mla_kv_norm[0], mla_w_ukv[0], even_w_out[0])
    fw = jnp.zeros((FNET_GROUPS, FNET_GROUP, FNET_GROUPS, FNET_GROUP), F32)
    for g in range(FNET_GROUPS):
        fw = fw.at[g, :, g, :].set(fnet_w[0, g])
    odd_wts = {
        "w_in": odd_w_in[0].astype(BF16),
        "short_w": short_w[0],
        "short_b": short_b[0].reshape(1, D_ZH),
        "dmat": jnp.asarray(_fnet_channel_dft(), BF16),
        "filt_w1": jnp.pad(filt_w1[0], ((0, FEAT_PAD - POS_EMB), (0, 0))),
        "filt_b1": filt_b1[0].reshape(1, FILTER_HIDDEN),
        "filt_w2": filt_w2[0],
        "filt_b2": filt_b2[0].reshape(1, FILTER_HIDDEN),
        "filt_w3": filt_w3[0],
        "filt_freq": filt_freq[0].reshape(1, FILTER_HIDDEN),
        "hyena_bias": hyena_bias[0],
        "fnet_w": fw.reshape(D_FNET, D_FNET).astype(BF16),
        "w_out_a": odd_w_out[0][:D_HYENA].astype(BF16),
        "w_out_b": odd_w_out[0][D_HYENA:].astype(BF16),
    }
    ffw = [(w_ff_in[l].astype(BF16), w_ff_out[l].astype(BF16)) for l in range(2)]
    y_prompt = _trunk(x_prompt, norm_mix, norm_mlp, ffw, even_wts, odd_wts)
    y_sample = _trunk(x_sample, norm_mix, norm_mlp, ffw, even_wts, odd_wts)
    return (y_prompt, y_sample)
```
